```python
import jax
import jax.numpy as jnp
from jax import lax

D_MODEL = 2048
BATCH = 4
SEQ = 4096
DEPTH = 1
DEC_BATCH = 128
DEC_SEQ = 4
PAST_LEN = 16384
PAGE_SIZE = 128

HG_HEADS = 8
HG_DK = 128
HG_DV = 128
HG_KWIDTH = HG_HEADS * HG_DK
HG_WIDTH = HG_HEADS * HG_DV
HG_CHUNK = 64
MLA_HEADS = 16
Q_LORA = 512
KV_LORA = 512
QK_NOPE = 128
QK_ROPE = 64
V_DIM = 128
MLA_WIDTH = MLA_HEADS * V_DIM
MLA_SCALE = (QK_NOPE + QK_ROPE) ** -0.5
ROPE_THETA = 10000.0
Q_BLOCK = 128
N_MEM = 256
MEM_HEADS = 4
MEM_DH = 256
MEM_WIDTH = MEM_HEADS * MEM_DH
N_EXPERTS = 32
TOP_K = 4
D_FF = 2048
SWIGLU_ALPHA = 1.702
SWIGLU_LIMIT = 7.0
MOE_BLOCK = 128
IN_SIZES = (Q_LORA, KV_LORA, QK_ROPE, HG_KWIDTH, HG_KWIDTH, HG_WIDTH, HG_WIDTH, MEM_WIDTH, 3 * D_MODEL)
IN_WIDTH = Q_LORA + KV_LORA + QK_ROPE + 2 * HG_KWIDTH + 2 * HG_WIDTH + MEM_WIDTH + 3 * D_MODEL
EPS = 1e-6
NEG_INF = -1e30
F32 = jnp.float32

kernel_name = 'hybrid_hgrn2_mla_mem_moe_step'


def rmsnorm(x, g):
    xf = x.astype(F32)
    y = xf * lax.rsqrt(jnp.mean(xf * xf, axis=-1, keepdims=True) + EPS)
    return (y * g.astype(F32)).astype(x.dtype)


def split_columns(z, sizes):
    parts, off = [], 0
    for n in sizes:
        parts.append(z[..., off:off + n])
        off += n
    return parts


def rope_cos_sin(pos):
    half = QK_ROPE // 2
    inv_freq = ROPE_THETA ** (-jnp.arange(half, dtype=F32) / half)
    ang = pos.astype(F32)[:, None] * inv_freq[None, :]
    return jnp.cos(ang), jnp.sin(ang)


def apply_rope(x, cos, sin):
    xf = x.astype(F32)
    x1, x2 = xf[..., :QK_ROPE // 2], xf[..., QK_ROPE // 2:]
    return jnp.concatenate([x1 * cos - x2 * sin, x1 * sin + x2 * cos], axis=-1).astype(x.dtype)


def hgrn_chunk(S, xs):
    q, k, v, logf = xs
    C = q.shape[1]
    G = jnp.cumsum(logf, axis=1)
    causal = jnp.tril(jnp.ones((C, C), dtype=bool))[None, :, :, None, None]
    decay = jnp.exp(jnp.where(causal, G[:, :, None] - G[:, None, :], -jnp.inf))
    A = jnp.einsum('bthk,bshk,btshk->bhts', q, k, decay)
    o = jnp.einsum('bhts,bshv->bthv', A, v) + jnp.einsum('bthk,bhkv->bthv', q * jnp.exp(G), S)
    G_end = G[:, -1]
    k_tail = k * jnp.exp(G_end[:, None] - G)
    S_new = jnp.exp(G_end)[..., None] * S + jnp.einsum('bshk,bshv->bhkv', k_tail, v)
    return S_new, o


def hgrn_recurrence(q, k, v, logf, S0):
    B, T = q.shape[:2]
    c = HG_CHUNK if T % HG_CHUNK == 0 else T
    n = T // c

    def to_chunks(z):
        return jnp.moveaxis(z.reshape(B, n, c, *z.shape[2:]), 1, 0)

    S_fin, o = lax.scan(hgrn_chunk, S0, (to_chunks(q), to_chunks(k), to_chunks(v), to_chunks(logf)))
    return S_fin, jnp.moveaxis(o, 0, 1).reshape(B, T, *o.shape[3:])


def mla_prompt_attend(q_nope, q_rope, ckv, k_rope, w_ukv_l):
    B, S = q_nope.shape[:2]
    kv = jnp.einsum('bsc,chd->bshd', ckv, w_ukv_l)
    k_nope, v = kv[..., :QK_NOPE], kv[..., QK_NOPE:]
    qb = min(Q_BLOCK, S)
    key_pos = jnp.arange(S)

    def block(i):
        q0 = i * qb
        qn = lax.dynamic_slice_in_dim(q_nope, q0, qb, axis=1)
        qr = lax.dynamic_slice_in_dim(q_rope, q0, qb, axis=1)
        s = (jnp.einsum('bqhd,bkhd->bhqk', qn, k_nope)
             + jnp.einsum('bqhr,bkr->bhqk', qr, k_rope)).astype(F32) * MLA_SCALE
        mask = (q0 + jnp.arange(qb))[:, None] >= key_pos[None, :]
        p = jax.nn.softmax(jnp.where(mask, s, NEG_INF), axis=-1)
        return jnp.einsum('bhqk,bkhd->bqhd', p.astype(v.dtype), v)

    o = lax.map(block, jnp.arange(S // qb))
    return jnp.moveaxis(o, 0, 1).reshape(B, S, MLA_HEADS, V_DIM)


def online_softmax_update(carry, s, vals):
    m, l, acc = carry
    m_new = jnp.maximum(m, jnp.max(s, axis=-1))
    alpha = jnp.exp(m - m_new)
    p = jnp.exp(s - m_new[..., None])
    l_new = l * alpha + jnp.sum(p, axis=-1)
    acc_new = acc * alpha[..., None] + jnp.einsum('bhtp,bpc->bhtc', p, vals.astype(F32))
    return (m_new, l_new, acc_new)


def mla_sample_attend(q_nope, q_rope, ckv, k_rope, cache_lat, cache_rope, layer, page_table, w_ukv_l):
    Bd, T = q_nope.shape[:2]
    w_uk, w_uv = w_ukv_l[..., :QK_NOPE], w_ukv_l[..., QK_NOPE:]
    q_lat = jnp.einsum('bthd,chd->bthc', q_nope, w_uk)

    def scores(lat, kr):
        return (jnp.einsum('bthc,bpc->bhtp', q_lat, lat)
                + jnp.einsum('bthr,bpr->bhtp', q_rope, kr)).astype(F32) * MLA_SCALE

    def page_step(carry, phys):
        lat = cache_lat[layer, phys]
        kr = cache_rope[layer, phys]
        return online_softmax_update(carry, scores(lat, kr), lat), None

    init = (jnp.full((Bd, MLA_HEADS, T), NEG_INF, F32),
            jnp.zeros((Bd, MLA_HEADS, T), F32),
            jnp.zeros((Bd, MLA_HEADS, T, KV_LORA), F32))
    carry, _ = lax.scan(page_step, init, page_table.T)
    causal = jnp.tril(jnp.ones((T, T), dtype=bool))
    s_new = jnp.where(causal, scores(ckv, k_rope), NEG_INF)
    m, l, acc = online_softmax_update(carry, s_new, ckv)
    o_lat = (acc / l[..., None]).astype(q_nope.dtype)
    return jnp.einsum('bhtc,chd->bthd', o_lat, w_uv)


def mem_attend(q, mem_k, mem_v):
    s = jnp.einsum('bthd,bnhd->bhtn', q, mem_k).astype(F32) * MEM_DH ** -0.5
    p = jax.nn.softmax(s, axis=-1).astype(mem_v.dtype)
    return jnp.einsum('bhtn,bnhd->bthd', p, mem_v)


def clamped_swiglu(h):
    x_glu = jnp.minimum(h[..., ::2], SWIGLU_LIMIT)
    x_lin = jnp.clip(h[..., 1::2], -SWIGLU_LIMIT, SWIGLU_LIMIT)
    return x_glu * jax.nn.sigmoid(SWIGLU_ALPHA * x_glu) * (x_lin + 1.0)


def moe_ffn(x2d, layer, w_router, b_router, w1, b1, w2, b2):
    T, D = x2d.shape
    logits = (x2d @ w_router[layer] + b_router[layer]).astype(F32)
    top_vals, top_idx = lax.top_k(logits, TOP_K)
    gates = jax.nn.softmax(top_vals, axis=-1)
    n_assign = T * TOP_K
    flat_e = top_idx.reshape(-1).astype(jnp.int32)
    tok = jnp.arange(n_assign, dtype=jnp.int32) // TOP_K
    order = jnp.argsort(flat_e)
    e_sorted = flat_e[order]
    counts = jnp.bincount(flat_e, length=N_EXPERTS).astype(jnp.int32)
    padded = (counts + MOE_BLOCK - 1) // MOE_BLOCK * MOE_BLOCK
    start = jnp.cumsum(counts) - counts
    end_pad = jnp.cumsum(padded)
    start_pad = end_pad - padded
    dest = (start_pad[e_sorted] + jnp.arange(n_assign, dtype=jnp.int32) - start[e_sorted]).astype(jnp.int32)
    n_blocks = -(-(n_assign + N_EXPERTS * (MOE_BLOCK - 1)) // MOE_BLOCK)
    n_rows = n_blocks * MOE_BLOCK
    row_tok = jnp.full((n_rows,), T, jnp.int32).at[dest].set(tok[order])
    x_pad = jnp.concatenate([x2d, jnp.zeros((1, D), x2d.dtype)], axis=0)
    xs = x_pad[row_tok].reshape(n_blocks, MOE_BLOCK, D)
    blk_e = jnp.minimum(jnp.searchsorted(end_pad, jnp.arange(n_blocks, dtype=jnp.int32) * MOE_BLOCK, side='right'),
                        N_EXPERTS - 1)

    def expert_block(args):
        xb, e = args
        hdn = xb @ w1[layer, e] + b1[layer, e]
        return clamped_swiglu(hdn) @ w2[layer, e] + b2[layer, e]

    ys = lax.map(expert_block, (xs, blk_e)).reshape(n_rows, D)
    dest_of_assign = jnp.zeros((n_assign,), jnp.int32).at[order].set(dest)
    y = ys[dest_of_assign].reshape(T, TOP_K, D)
    return jnp.einsum('tk,tkd->td', gates.astype(y.dtype), y)


def setup_inputs(seed: int = 0) -> dict:
    key = jax.random.key(seed)
    ks = iter(jax.random.split(key, 48))

    def nrm(shape, scale=1.0):
        return jax.random.normal(next(ks), shape, F32) * scale

    def gain(shape):
        return 1.0 + 0.05 * jax.random.normal(next(ks), shape, F32)

    n_pages = PAST_LEN // PAGE_SIZE
    n_used = DEC_BATCH * n_pages
    n_phys = n_used + (n_used + 3) // 4
    page_table = jax.random.permutation(next(ks), n_phys)[:n_used].reshape(DEC_BATCH, n_pages).astype(jnp.int32)
    return {
        'x_prompt': nrm((BATCH, SEQ, D_MODEL)),
        'x_sample': nrm((DEC_BATCH, DEC_SEQ, D_MODEL)),
        'cache_kv_latent': nrm((DEPTH, n_phys, PAGE_SIZE, KV_LORA)),
        'cache_k_rope': nrm((DEPTH, n_phys, PAGE_SIZE, QK_ROPE)),
        'state_hgrn': nrm((DEPTH, DEC_BATCH, HG_HEADS, HG_DK, HG_DV), 0.5),
        'cache_mem_k': nrm((DEPTH, DEC_BATCH, N_MEM, MEM_HEADS, MEM_DH)),
        'cache_mem_v': nrm((DEPTH, DEC_BATCH, N_MEM, MEM_HEADS, MEM_DH)),
        'page_table': page_table,
        'mem_prompt': nrm((BATCH, N_MEM, D_MODEL)),
        'g_mix': gain((DEPTH, D_MODEL)),
        'w_in': nrm((DEPTH, D_MODEL, IN_WIDTH), D_MODEL ** -0.5),
        'hg_lb_logits': nrm((DEPTH + 1, HG_KWIDTH), 0.1),
        'g_hg_out': gain((DEPTH, HG_DV)),
        'g_q_lora': gain((DEPTH, Q_LORA)),
        'w_uq': nrm((DEPTH, Q_LORA, MLA_HEADS, QK_NOPE + QK_ROPE), Q_LORA ** -0.5),
        'g_kv_lora': gain((DEPTH, KV_LORA)),
        'w_ukv': nrm((DEPTH, KV_LORA, MLA_HEADS, QK_NOPE + V_DIM), KV_LORA ** -0.5),
        'g_mem': gain((DEPTH, D_MODEL)),
        'w_mem_kv': nrm((DEPTH, D_MODEL, 2 * MEM_WIDTH), D_MODEL ** -0.5),
        'w_br_hg': nrm((DEPTH, HG_WIDTH, D_MODEL), HG_WIDTH ** -0.5),
        'w_br_mla': nrm((DEPTH, MLA_WIDTH, D_MODEL), MLA_WIDTH ** -0.5),
        'w_br_mem': nrm((DEPTH, MEM_WIDTH, D_MODEL), MEM_WIDTH ** -0.5),
        'w_out': nrm((DEPTH, D_MODEL, D_MODEL), D_MODEL ** -0.5),
        'g_ffn': gain((DEPTH, D_MODEL)),
        'w_router': nrm((DEPTH, D_MODEL, N_EXPERTS), D_MODEL ** -0.5),
        'b_router': nrm((DEPTH, N_EXPERTS), 0.01),
        'w1': nrm((DEPTH, N_EXPERTS, D_MODEL, 2 * D_FF), D_MODEL ** -0.5),
        'b1': nrm((DEPTH, N_EXPERTS, 2 * D_FF), 0.01),
        'w2': nrm((DEPTH, N_EXPERTS, D_FF, D_MODEL), D_FF ** -0.5),
        'b2': nrm((DEPTH, N_EXPERTS, D_MODEL), 0.01),
        'g_final': gain((D_MODEL,)),
    }


def reference(x_prompt, x_sample, cache_kv_latent, cache_k_rope, state_hgrn, cache_mem_k, cache_mem_v,
              page_table, mem_prompt, g_mix, w_in, hg_lb_logits, g_hg_out, g_q_lora, w_uq, g_kv_lora, w_ukv,
              g_mem, w_mem_kv, w_br_hg, w_br_mla, w_br_mem, w_out, g_ffn, w_router, b_router, w1, b1, w2, b2,
              g_final):
    B, S, _ = x_prompt.shape
    Bd, T, _ = x_sample.shape
    past_len = page_table.shape[1] * cache_kv_latent.shape[2]
    pos_p = jnp.arange(S)
    pos_s = past_len + jnp.arange(T)
    lower_bounds = jnp.cumsum(jax.nn.softmax(hg_lb_logits.astype(F32), axis=0), axis=0)
    h_p, h_s = x_prompt, x_sample
    lat_p, rope_p, lat_s, rope_s, hg_p, hg_s, mk_list, mv_list = [], [], [], [], [], [], [], []
    for l in range(DEPTH):
        lb = lower_bounds[l]

        def mixer(a, pos, S0, attend, mem_k, mem_v):
            Bq, Tq = a.shape[:2]
            c_q, c_kv, k_pe, hq, hf, hi, hgate, mq, gates = split_columns(a @ w_in[l], IN_SIZES)
            cos, sin = rope_cos_sin(pos)
            q = jnp.einsum('btc,chd->bthd', rmsnorm(c_q, g_q_lora[l]), w_uq[l])
            q_nope = q[..., :QK_NOPE]
            q_rope = apply_rope(q[..., QK_NOPE:], cos[:, None], sin[:, None])
            ckv = rmsnorm(c_kv, g_kv_lora[l])
            k_rope = apply_rope(k_pe, cos, sin)
            o_mla = attend(q_nope, q_rope, ckv, k_rope).reshape(Bq, Tq, MLA_WIDTH)
            f = lb + (1.0 - lb) * jax.nn.sigmoid(hf.astype(F32))

            def heads_k(z):
                return z.reshape(Bq, Tq, HG_HEADS, HG_DK)

            def heads_v(z):
                return z.reshape(Bq, Tq, HG_HEADS, HG_DV)

            S_fin, o = hgrn_recurrence(heads_k(jax.nn.silu(hq.astype(F32)) * HG_DK ** -0.5),
                                       heads_k(1.0 - f), heads_v(hi.astype(F32)),
                                       heads_k(jnp.log(f)), S0)
            o_hg = (rmsnorm(o, g_hg_out[l]) * jax.nn.silu(heads_v(hgate.astype(F32)))
                    ).reshape(Bq, Tq, HG_WIDTH).astype(a.dtype)
            o_mem = mem_attend(mq.reshape(Bq, Tq, MEM_HEADS, MEM_DH), mem_k, mem_v).reshape(Bq, Tq, MEM_WIDTH)
            gate_hg, gate_mla, gate_mem = split_columns(gates, (D_MODEL, D_MODEL, D_MODEL))
            merged = (jax.nn.sigmoid(gate_hg) * (o_hg @ w_br_hg[l])
                      + jax.nn.sigmoid(gate_mla) * (o_mla @ w_br_mla[l])
                      + jax.nn.sigmoid(gate_mem) * (o_mem @ w_br_mem[l]))
            return merged @ w_out[l], S_fin, ckv, k_rope

        mem_kv = rmsnorm(mem_prompt, g_mem[l]) @ w_mem_kv[l]
        mk_p = mem_kv[..., :MEM_WIDTH].reshape(B, -1, MEM_HEADS, MEM_DH)
        mv_p = mem_kv[..., MEM_WIDTH:].reshape(B, -1, MEM_HEADS, MEM_DH)
        S0_p = jnp.zeros((B, HG_HEADS, HG_DK, HG_DV), F32)
        mix_p, S_p, ckv_p, kr_p = mixer(
            rmsnorm(h_p, g_mix[l]), pos_p, S0_p,
            lambda qn, qr, ckv, kr: mla_prompt_attend(qn, qr, ckv, kr, w_ukv[l]), mk_p, mv_p)
        h_p = h_p + mix_p
        h_p = h_p + moe_ffn(rmsnorm(h_p, g_ffn[l]).reshape(B * S, -1), l,
                            w_router, b_router, w1, b1, w2, b2).reshape(h_p.shape)
        mix_s, S_s, ckv_s, kr_s = mixer(
            rmsnorm(h_s, g_mix[l]), pos_s, state_hgrn[l].astype(F32),
            lambda qn, qr, ckv, kr: mla_sample_attend(qn, qr, ckv, kr, cache_kv_latent, cache_k_rope, l,
                                                      page_table, w_ukv[l]),
            cache_mem_k[l], cache_mem_v[l])
        h_s = h_s + mix_s
        h_s = h_s + moe_ffn(rmsnorm(h_s, g_ffn[l]).reshape(Bd * T, -1), l,
                            w_router, b_router, w1, b1, w2, b2).reshape(h_s.shape)
        lat_p.append(ckv_p)
        rope_p.append(kr_p)
        lat_s.append(ckv_s)
        rope_s.append(kr_s)
        hg_p.append(S_p)
        hg_s.append(S_s)
        mk_list.append(mk_p)
        mv_list.append(mv_p)
    y_prompt = rmsnorm(h_p, g_final)
    y_sample = rmsnorm(h_s, g_final)
    new_kv_latent_prompt = jnp.stack(lat_p)
    new_k_rope_prompt = jnp.stack(rope_p)
    new_kv_latent_sample = jnp.stack(lat_s)
    new_k_rope_sample = jnp.stack(rope_s)
    new_state_hgrn_prompt = jnp.stack(hg_p).astype(x_prompt.dtype)
    new_state_hgrn_sample = jnp.stack(hg_s).astype(state_hgrn.dtype)
    new_mem_k_prompt = jnp.stack(mk_list)
    new_mem_v_prompt = jnp.stack(mv_list)
    return (y_prompt, y_sample, new_kv_latent_prompt, new_k_rope_prompt, new_kv_latent_sample,
            new_k_rope_sample, new_state_hgrn_prompt, new_state_hgrn_sample, new_mem_k_prompt, new_mem_v_prompt)
```

```python
import functools
import math

import numpy as np
import jax
import jax.numpy as jnp
from jax import lax
from jax.experimental import pallas as pl
from jax.experimental.pallas import tpu as pltpu

F32 = jnp.float32
BF16 = jnp.bfloat16

EPS = 1e-6
NEG_INF = -1e30
ROPE_THETA = 10000.0
SWIGLU_ALPHA = 1.702
SWIGLU_LIMIT = 7.0
TOP_K = 4

LANE = 128
VMEM_LIMIT = 56 * 1024 * 1024

HG_CHUNK = 64
MOE_ROWS = 1024
MOE_SUB = 256
MOE_TF = 256
PAGES_PER_STEP = 8


def _params(*sem):
    return pltpu.CompilerParams(dimension_semantics=sem, vmem_limit_bytes=VMEM_LIMIT)


def _dot(a, b):
    return jnp.dot(a, b, preferred_element_type=F32)


def _dot_nt(a, b):
    return lax.dot_general(a, b, (((1,), (1,)), ((), ())), preferred_element_type=F32)


def _dot_tn(a, b):
    return lax.dot_general(a, b, (((0,), (0,)), ((), ())), preferred_element_type=F32)


def _sigmoid(x):
    return 1.0 / (1.0 + jnp.exp(-x))


def _rmsnorm_kernel(x_ref, g_ref, o_ref):
    x = x_ref[...].astype(F32)
    ms = jnp.mean(x * x, axis=-1, keepdims=True)
    o_ref[...] = (x * lax.rsqrt(ms + EPS) * g_ref[...]).astype(o_ref.dtype)


def _rmsnorm(x, g, out_dtype, tm=512):
    m, d = x.shape
    return pl.pallas_call(
        _rmsnorm_kernel,
        grid=(m // tm,),
        in_specs=[pl.BlockSpec((tm, d), lambda i: (i, 0)),
                  pl.BlockSpec((1, d), lambda i: (0, 0))],
        out_specs=pl.BlockSpec((tm, d), lambda i: (i, 0)),
        out_shape=jax.ShapeDtypeStruct((m, d), out_dtype),
        compiler_params=_params("parallel"),
        name="rmsnorm",
    )(x, g.reshape(1, d).astype(F32))


def _matmul_kernel(a_ref, w_ref, o_ref):
    o_ref[...] = _dot(a_ref[...], w_ref[...]).astype(o_ref.dtype)


def _matmul(a, w, out_dtype, tm, tn):
    m, k = a.shape
    n = w.shape[1]
    return pl.pallas_call(
        _matmul_kernel,
        grid=(m // tm, n // tn),
        in_specs=[pl.BlockSpec((tm, k), lambda i, j: (i, 0)),
                  pl.BlockSpec((k, tn), lambda i, j: (0, j))],
        out_specs=pl.BlockSpec((tm, tn), lambda i, j: (i, j)),
        out_shape=jax.ShapeDtypeStruct((m, n), out_dtype),
        compiler_params=_params("parallel", "parallel"),
        name="matmul",
    )(a, w)


def _mla_prep_kernel(cq_ref, ckv_ref, kpe_ref, kpesw_ref, gq_ref, gkv_ref, cos_ref, sin_ref,
                     cqn_ref, ckvf_ref, ckvb_ref, krf_ref, krb_ref):
    cq = cq_ref[...]
    cqn = cq * lax.rsqrt(jnp.mean(cq * cq, axis=-1, keepdims=True) + EPS) * gq_ref[...]
    cqn_ref[...] = cqn.astype(BF16)
    ckv = ckv_ref[...]
    ckvn = ckv * lax.rsqrt(jnp.mean(ckv * ckv, axis=-1, keepdims=True) + EPS) * gkv_ref[...]
    ckvf_ref[...] = ckvn
    ckvb_ref[...] = ckvn.astype(BF16)
    kr = kpe_ref[...] * cos_ref[...] + kpesw_ref[...] * sin_ref[...]
    krf_ref[...] = kr[:, :krf_ref.shape[1]]
    krb_ref[...] = kr.astype(BF16)


def _mla_prep(z, col0, g_q, g_kv, cos_t, sin_t, n_pos_blocks, tm, q_lora, kv_lora, rope):
    m = z.shape[0]
    cq_blk = col0 // q_lora
    kpe_blk = (col0 + q_lora + kv_lora) // LANE
    row = lambda i: (i, 0)
    return pl.pallas_call(
        _mla_prep_kernel,
        grid=(m // tm,),
        in_specs=[pl.BlockSpec((tm, q_lora), lambda i: (i, cq_blk)),
                  pl.BlockSpec((tm, kv_lora), lambda i: (i, cq_blk + 1)),
                  pl.BlockSpec((tm, LANE), lambda i: (i, kpe_blk)),
                  pl.BlockSpec((tm, LANE), lambda i: (i, kpe_blk + 1)),
                  pl.BlockSpec((1, q_lora), lambda i: (0, 0)),
                  pl.BlockSpec((1, kv_lora), lambda i: (0, 0)),
                  pl.BlockSpec((tm, LANE), lambda i: (i % n_pos_blocks, 0)),
                  pl.BlockSpec((tm, LANE), lambda i: (i % n_pos_blocks, 0))],
        out_specs=[pl.BlockSpec((tm, q_lora), row),
                   pl.BlockSpec((tm, kv_lora), row),
                   pl.BlockSpec((tm, kv_lora), row),
                   pl.BlockSpec((tm, rope), row),
                   pl.BlockSpec((tm, LANE), row)],
        out_shape=[jax.ShapeDtypeStruct((m, q_lora), BF16),
                   jax.ShapeDtypeStruct((m, kv_lora), F32),
                   jax.ShapeDtypeStruct((m, kv_lora), BF16),
                   jax.ShapeDtypeStruct((m, rope), F32),
                   jax.ShapeDtypeStruct((m, LANE), BF16)],
        compiler_params=_params("parallel"),
        name="mla_prep",
    )(z, z, z, z, g_q.reshape(1, -1), g_kv.reshape(1, -1), cos_t, sin_t)


def _q_up_kernel(a_ref, w_ref, wsw_ref, cos_ref, sin_ref, o_ref, *, heads, scale):
    a = a_ref[...]
    cos = cos_ref[...]
    sin = sin_ref[...]
    for h in range(heads):
        q = _dot(a, w_ref[:, h * 2 * LANE:(h + 1) * 2 * LANE])
        qsw = _dot(a, wsw_ref[:, h * LANE:(h + 1) * LANE])
        o_ref[:, h * 2 * LANE:h * 2 * LANE + LANE] = (q[:, :LANE] * scale).astype(o_ref.dtype)
        o_ref[:, h * 2 * LANE + LANE:(h + 1) * 2 * LANE] = (
            (q[:, LANE:] * cos + qsw * sin) * scale).astype(o_ref.dtype)


def _q_up(cqn, w, wsw, cos_t, sin_t, n_pos_blocks, tm, heads, scale):
    m, k = cqn.shape
    return pl.pallas_call(
        functools.partial(_q_up_kernel, heads=heads, scale=scale),
        grid=(m // tm,),
        in_specs=[pl.BlockSpec((tm, k), lambda i: (i, 0)),
                  pl.BlockSpec(w.shape, lambda i: (0, 0)),
                  pl.BlockSpec(wsw.shape, lambda i: (0, 0)),
                  pl.BlockSpec((tm, LANE), lambda i: (i % n_pos_blocks, 0)),
                  pl.BlockSpec((tm, LANE), lambda i: (i % n_pos_blocks, 0))],
        out_specs=pl.BlockSpec((tm, heads * 2 * LANE), lambda i: (i, 0)),
        out_shape=jax.ShapeDtypeStruct((m, heads * 2 * LANE), BF16),
        compiler_params=_params("parallel"),
        name="mla_q_up",
    )(cqn, w, wsw, cos_t, sin_t)


def _kv_up_kernel(a_ref, w_ref, kr_ref, k_ref, v_ref, *, heads):
    a = a_ref[...]
    kr = kr_ref[...]
    for h in range(heads):
        kv = _dot(a, w_ref[:, h * 2 * LANE:(h + 1) * 2 * LANE])
        k_ref[:, h * 2 * LANE:h * 2 * LANE + LANE] = kv[:, :LANE].astype(k_ref.dtype)
        k_ref[:, h * 2 * LANE + LANE:(h + 1) * 2 * LANE] = kr
        v_ref[:, h * LANE:(h + 1) * LANE] = kv[:, LANE:].astype(v_ref.dtype)


def _kv_up(ckvb, w, krb, tm, heads):
    m, k = ckvb.shape
    return pl.pallas_call(
        functools.partial(_kv_up_kernel, heads=heads),
        grid=(m // tm,),
        in_specs=[pl.BlockSpec((tm, k), lambda i: (i, 0)),
                  pl.BlockSpec(w.shape, lambda i: (0, 0)),
                  pl.BlockSpec((tm, LANE), lambda i: (i, 0))],
        out_specs=[pl.BlockSpec((tm, heads * 2 * LANE), lambda i: (i, 0)),
                   pl.BlockSpec((tm, heads * LANE), lambda i: (i, 0))],
        out_shape=[jax.ShapeDtypeStruct((m, heads * 2 * LANE), BF16),
                   jax.ShapeDtypeStruct((m, heads * LANE), BF16)],
        compiler_params=_params("parallel"),
        name="mla_kv_up",
    )(ckvb, w, krb)


def _flash_kernel(q_ref, k_ref, v_ref, o_ref, m_scr, l_scr, acc_scr, *, tq):
    qi = pl.program_id(2)
    q = q_ref[...]
    m_scr[...] = jnp.full(m_scr.shape, NEG_INF, F32)
    l_scr[...] = jnp.zeros(l_scr.shape, F32)
    acc_scr[...] = jnp.zeros(acc_scr.shape, F32)

    def update(kj, masked):
        k = k_ref[pl.ds(kj * tq, tq), :]
        v = v_ref[pl.ds(kj * tq, tq), :]
        s = _dot_nt(q, k)
        if masked:
            r = lax.broadcasted_iota(jnp.int32, s.shape, 0)
            c = lax.broadcasted_iota(jnp.int32, s.shape, 1)
            s = jnp.where(r >= c, s, NEG_INF)
        m_old = m_scr[...]
        m_new = jnp.maximum(m_old, jnp.max(s, axis=-1, keepdims=True))
        alpha = jnp.exp(m_old - m_new)
        p = jnp.exp(s - m_new)
        l_scr[...] = alpha * l_scr[...] + jnp.sum(p, axis=-1, keepdims=True)
        acc_scr[...] = alpha * acc_scr[...] + _dot(p.astype(BF16), v)
        m_scr[...] = m_new

    def body(kj, carry):
        update(kj, False)
        return carry

    lax.fori_loop(0, qi, body, 0)
    update(qi, True)
    o_ref[...] = (acc_scr[...] / l_scr[...]).astype(o_ref.dtype)


def _flash(q_cat, k_cat, v, batch, seq, heads, tq):
    nq = seq // tq
    t = batch * seq
    return pl.pallas_call(
        functools.partial(_flash_kernel, tq=tq),
        grid=(batch, heads, nq),
        in_specs=[pl.BlockSpec((tq, 2 * LANE), lambda b, h, i: (b * nq + i, h)),
                  pl.BlockSpec((seq, 2 * LANE), lambda b, h, i: (b, h)),
                  pl.BlockSpec((seq, LANE), lambda b, h, i: (b, h))],
        out_specs=pl.BlockSpec((tq, LANE), lambda b, h, i: (b * nq + i, h)),
        out_shape=jax.ShapeDtypeStruct((t, heads * LANE), BF16),
        scratch_shapes=[pltpu.VMEM((tq, 1), F32), pltpu.VMEM((tq, 1), F32), pltpu.VMEM((tq, LANE), F32)],
        compiler_params=_params("parallel", "parallel", "arbitrary"),
        name="mla_flash",
    )(q_cat, k_cat, v)


def _q_lat_kernel(x_ref, w_ref, o_ref, *, kv_lora):
    x = x_ref[...]
    o_ref[:, :kv_lora] = _dot_nt(x[:, :LANE], w_ref[...]).astype(o_ref.dtype)
    o_ref[:, kv_lora:] = x[:, LANE:]


def _q_lat(q_cat, w_uk, heads, kv_lora):
    m = q_cat.shape[0]
    return pl.pallas_call(
        functools.partial(_q_lat_kernel, kv_lora=kv_lora),
        grid=(heads,),
        in_specs=[pl.BlockSpec((m, 2 * LANE), lambda h: (0, h)),
                  pl.BlockSpec((None, kv_lora, LANE), lambda h: (h, 0, 0))],
        out_specs=pl.BlockSpec((None, m, kv_lora + LANE), lambda h: (h, 0, 0)),
        out_shape=jax.ShapeDtypeStruct((heads, m, kv_lora + LANE), BF16),
        compiler_params=_params("parallel"),
        name="mla_q_absorb",
    )(q_cat, w_uk)


def _paged_kernel(pt_ref, q_ref, *refs, pages, kv_lora, rope, t_new):
    lat_refs = refs[:pages]
    rope_refs = refs[pages:2 * pages]
    nlat_ref, nrope_ref, o_ref, m_scr, l_scr, acc_scr = refs[2 * pages:]
    g = pl.program_id(1)

    @pl.when(g == 0)
    def _():
        m_scr[...] = jnp.full(m_scr.shape, NEG_INF, F32)
        l_scr[...] = jnp.zeros(l_scr.shape, F32)
        acc_scr[...] = jnp.zeros(acc_scr.shape, F32)

    q = q_ref[...]
    q_lat = q[:, :kv_lora]
    q_rope = q[:, kv_lora:kv_lora + rope]

    def update(s_list, lat_list):
        m_old = m_scr[...]
        m_new = m_old
        for s in s_list:
            m_new = jnp.maximum(m_new, jnp.max(s, axis=-1, keepdims=True))
        alpha = jnp.exp(m_old - m_new)
        l_new = alpha * l_scr[...]
        acc = alpha * acc_scr[...]
        for s, lat in zip(s_list, lat_list):
            p = jnp.exp(s - m_new)
            l_new = l_new + jnp.sum(p, axis=-1, keepdims=True)
            acc = acc + _dot(p.astype(BF16), lat)
        m_scr[...] = m_new
        l_scr[...] = l_new
        acc_scr[...] = acc

    s_list, lat_list = [], []
    for i in range(pages):
        lat = lat_refs[i][...].astype(BF16)
        kr = rope_refs[i][...].astype(BF16)
        s_list.append(_dot_nt(q_lat, lat) + _dot_nt(q_rope, kr))
        lat_list.append(lat)
    update(s_list, lat_list)

    @pl.when(g == pl.num_programs(1) - 1)
    def _():
        lat = nlat_ref[...]
        s = _dot_nt(q_lat, lat) + _dot_nt(q_rope, nrope_ref[...][:, :rope])
        r = lax.broadcasted_iota(jnp.int32, s.shape, 0) % t_new
        c = lax.broadcasted_iota(jnp.int32, s.shape, 1)
        s = jnp.where(c <= r, s, NEG_INF)
        update([s], [lat])
        o_ref[...] = (acc_scr[...] / l_scr[...]).astype(o_ref.dtype)


def _paged_attention(q_all, cache_lat, cache_rope, layer, page_table, new_lat, new_rope, t_new):
    bd, rows, qw = q_all.shape
    _, _, page, kv_lora = cache_lat.shape
    rope = cache_rope.shape[-1]
    n_pages = page_table.shape[1]
    pages = PAGES_PER_STEP
    groups = n_pages // pages

    def lat_spec(i):
        return pl.BlockSpec((None, None, page, kv_lora),
                            lambda b, g, pt: (layer, pt[b * n_pages + g * pages + i], 0, 0))

    def rope_spec(i):
        return pl.BlockSpec((None, None, page, rope),
                            lambda b, g, pt: (layer, pt[b * n_pages + g * pages + i], 0, 0))

    grid_spec = pltpu.PrefetchScalarGridSpec(
        num_scalar_prefetch=1,
        grid=(bd, groups),
        in_specs=([pl.BlockSpec((None, rows, qw), lambda b, g, pt: (b, 0, 0))]
                  + [lat_spec(i) for i in range(pages)]
                  + [rope_spec(i) for i in range(pages)]
                  + [pl.BlockSpec((None, page, kv_lora), lambda b, g, pt: (b, 0, 0)),
                     pl.BlockSpec((None, page, LANE), lambda b, g, pt: (b, 0, 0))]),
        out_specs=pl.BlockSpec((None, rows, kv_lora), lambda b, g, pt: (b, 0, 0)),
        scratch_shapes=[pltpu.VMEM((rows, 1), F32), pltpu.VMEM((rows, 1), F32),
                        pltpu.VMEM((rows, kv_lora), F32)],
    )
    return pl.pallas_call(
        functools.partial(_paged_kernel, pages=pages, kv_lora=kv_lora, rope=rope, t_new=t_new),
        grid_spec=grid_spec,
        out_shape=jax.ShapeDtypeStruct((bd, rows, kv_lora), BF16),
        compiler_params=_params("parallel", "arbitrary"),
        name="mla_paged",
    )(page_table.reshape(-1), q_all, *([cache_lat] * pages), *([cache_rope] * pages), new_lat, new_rope)


def _head_matmul_kernel(x_ref, w_ref, o_ref):
    o_ref[...] = _dot(x_ref[...], w_ref[...]).astype(o_ref.dtype)


def _head_matmul(x, w):
    heads, m, k = x.shape
    return pl.pallas_call(
        _head_matmul_kernel,
        grid=(heads,),
        in_specs=[pl.BlockSpec((None, m, k), lambda h: (h, 0, 0)),
                  pl.BlockSpec((None, k, LANE), lambda h: (h, 0, 0))],
        out_specs=pl.BlockSpec((m, LANE), lambda h: (0, h)),
        out_shape=jax.ShapeDtypeStruct((m, heads * LANE), BF16),
        compiler_params=_params("parallel"),
        name="mla_v_up",
    )(x, w)


def _hgrn_constants(c):
    levels = int(math.log2(c))
    tril = np.tril(np.ones((c, c), np.float32))
    mats = [tril]
    masks = [np.eye(c, dtype=np.float32)]
    idx = np.arange(c)
    m = c // 2
    for _ in range(levels):
        blk = idx // (2 * m)
        mid = blk * 2 * m + m - 1
        sel = np.zeros((c, c), np.float32)
        sel[idx, mid] = 1.0
        mats.append(sel @ tril)
        second = (idx % (2 * m)) >= m
        mask = (blk[:, None] == blk[None, :]) & second[:, None] & (~second)[None, :]
        masks.append(mask.astype(np.float32))
        m //= 2
    return np.concatenate(mats, axis=0), np.stack(masks, axis=0)


def _hgrn_kernel(*refs, c, n_chunks, hpb, valid_rows, has_s0, levels, qscale):
    if has_s0:
        cm_ref, mask_ref, lb_ref, g_ref, hq_ref, hf_ref, hi_ref, hg_ref, s0_ref, o_ref, sfin_ref, st_scr = refs
    else:
        cm_ref, mask_ref, lb_ref, g_ref, hq_ref, hf_ref, hi_ref, hg_ref, o_ref, sfin_ref, st_scr = refs
    ti = pl.program_id(2)

    @pl.when(ti == 0)
    def _():
        for hh in range(hpb):
            if has_s0:
                st_scr[hh] = s0_ref[hh].T
            else:
                st_scr[hh] = jnp.zeros(st_scr.shape[1:], F32)

    cm = cm_ref[...]
    g_out = g_ref[...]

    def chunk(ci, carry):
        rows = pl.ds(pl.multiple_of(ci * c, c), c)
        for hh in range(hpb):
            cols = slice(hh * LANE, (hh + 1) * LANE)
            lb = lb_ref[:, cols]
            f = lb + (1.0 - lb) * _sigmoid(hf_ref[rows, cols])
            logf = jnp.log(f)
            kk = 1.0 - f
            if valid_rows < c:
                live = lax.broadcasted_iota(jnp.int32, logf.shape, 0) < valid_rows
                logf = jnp.where(live, logf, 0.0)
                kk = jnp.where(live, kk, 0.0)
            hq = hq_ref[rows, cols]
            qs = hq * _sigmoid(hq) * qscale
            v = hi_ref[rows, cols].astype(BF16)
            gr = jnp.dot(cm, logf, precision=lax.Precision.HIGHEST, preferred_element_type=F32)
            gc = gr[0:c]
            g_end = gc[c - 1:c]
            st = st_scr[hh]
            o = _dot_nt((qs * jnp.exp(gc)).astype(BF16), st.astype(BF16))
            a = mask_ref[0] * _dot_nt(qs.astype(BF16), kk.astype(BF16))
            for lv in range(1, levels + 1):
                d = gc - gr[lv * c:(lv + 1) * c]
                qt = qs * jnp.exp(jnp.minimum(d, 0.0))
                kt = kk * jnp.exp(jnp.minimum(-d, 0.0))
                a = a + mask_ref[lv] * _dot_nt(qt.astype(BF16), kt.astype(BF16))
            o = o + _dot(a.astype(BF16), v)
            k_tail = kk * jnp.exp(g_end - gc)
            st_scr[hh] = st * jnp.exp(g_end) + _dot_tn(v, k_tail.astype(BF16))
            on = o * lax.rsqrt(jnp.mean(o * o, axis=-1, keepdims=True) + EPS) * g_out
            hg = hg_ref[rows, cols]
            o_ref[rows, cols] = (on * (hg * _sigmoid(hg))).astype(o_ref.dtype)
        return carry

    lax.fori_loop(0, n_chunks, chunk, 0)

    @pl.when(ti == pl.num_programs(2) - 1)
    def _():
        for hh in range(hpb):
            sfin_ref[hh] = st_scr[hh].T


def _hgrn(z, col0, lb, g_out, s0, batch, rows_per_batch, heads, c, tt, hpb, valid_rows):
    n_tiles = rows_per_batch // tt
    width = heads * LANE
    bw = hpb * LANE
    hgroups = heads // hpb
    cm, masks = _hgrn_constants(c)
    levels = masks.shape[0] - 1
    has_s0 = s0 is not None

    def zspec(k):
        base = (col0 + k * width) // bw
        return pl.BlockSpec((tt, bw), lambda b, hg, i: (b * n_tiles + i, base + hg))

    in_specs = [pl.BlockSpec(cm.shape, lambda b, hg, i: (0, 0)),
                pl.BlockSpec(masks.shape, lambda b, hg, i: (0, 0, 0)),
                pl.BlockSpec((1, bw), lambda b, hg, i: (0, hg)),
                pl.BlockSpec((1, LANE), lambda b, hg, i: (0, 0)),
                zspec(0), zspec(1), zspec(2), zspec(3)]
    args = [jnp.asarray(cm), jnp.asarray(masks), lb.reshape(1, width), g_out.reshape(1, LANE), z, z, z, z]
    if has_s0:
        in_specs.append(pl.BlockSpec((None, hpb, LANE, LANE), lambda b, hg, i: (b, hg, 0, 0)))
        args.append(s0)
    return pl.pallas_call(
        functools.partial(_hgrn_kernel, c=c, n_chunks=tt // c, hpb=hpb, valid_rows=valid_rows,
                          has_s0=has_s0, levels=levels, qscale=float(LANE) ** -0.5),
        grid=(batch, hgroups, n_tiles),
        in_specs=in_specs,
        out_specs=[pl.BlockSpec((tt, bw), lambda b, hg, i: (b * n_tiles + i, hg)),
                   pl.BlockSpec((None, hpb, LANE, LANE), lambda b, hg, i: (b, hg, 0, 0))],
        out_shape=[jax.ShapeDtypeStruct((batch * rows_per_batch, width), BF16),
                   jax.ShapeDtypeStruct((batch, heads, LANE, LANE), F32)],
        scratch_shapes=[pltpu.VMEM((hpb, LANE, LANE), F32)],
        compiler_params=_params("parallel", "parallel", "arbitrary"),
        name="hgrn",
    )(*args)


def _mem_attn_kernel(q_ref, k_ref, v_ref, o_ref, *, heads, dh, rows_per_batch, n_mem, groups):
    scale = float(dh) ** -0.5
    for h in range(heads):
        cols = slice(h * dh, (h + 1) * dh)
        q = (q_ref[:, cols] * scale).astype(BF16)
        k = k_ref[:, cols].astype(BF16)
        v = v_ref[:, cols].astype(BF16)
        s = _dot_nt(q, k)
        if groups > 1:
            r = lax.broadcasted_iota(jnp.int32, s.shape, 0) // rows_per_batch
            cidx = lax.broadcasted_iota(jnp.int32, s.shape, 1) // n_mem
            s = jnp.where(r == cidx, s, NEG_INF)
        m = jnp.max(s, axis=-1, keepdims=True)
        p = jnp.exp(s - m)
        p = (p / jnp.sum(p, axis=-1, keepdims=True)).astype(BF16)
        o_ref[:, cols] = _dot(p, v).astype(o_ref.dtype)


def _mem_attn(z, q_col0, k_arr, k_blk, v_arr, v_blk, n_batches, rows_per_batch, n_mem, heads, dh, tq, groups):
    width = heads * dh
    qb = q_col0 // width
    rows = z.shape[0]
    if groups > 1:
        grid = (n_batches // groups, 1)
        tq = groups * rows_per_batch
        nq = 1
    else:
        nq = rows_per_batch // tq
        grid = (n_batches, nq)
    return pl.pallas_call(
        functools.partial(_mem_attn_kernel, heads=heads, dh=dh, rows_per_batch=rows_per_batch,
                          n_mem=n_mem, groups=groups),
        grid=grid,
        in_specs=[pl.BlockSpec((tq, width), lambda b, i: (b * nq + i, qb)),
                  pl.BlockSpec((groups * n_mem, width), lambda b, i: (b, k_blk)),
                  pl.BlockSpec((groups * n_mem, width), lambda b, i: (b, v_blk))],
        out_specs=pl.BlockSpec((tq, width), lambda b, i: (b * nq + i, 0)),
        out_shape=jax.ShapeDtypeStruct((rows, width), BF16),
        compiler_params=_params("parallel", "parallel"),
        name="mem_attn",
    )(z, k_arr, v_arr)


def _merge_kernel(ohg_ref, omla_ref, omem_ref, g1_ref, g2_ref, g3_ref, w1_ref, w2_ref, w3_ref, o_ref):
    acc = _sigmoid(g1_ref[...]) * _dot(ohg_ref[...], w1_ref[...])
    acc = acc + _sigmoid(g2_ref[...]) * _dot(omla_ref[...], w2_ref[...])
    acc = acc + _sigmoid(g3_ref[...]) * _dot(omem_ref[...], w3_ref[...])
    o_ref[...] = acc.astype(o_ref.dtype)


def _merge(o_hg, o_mla, o_mem, z, gate_col0, w_hg, w_mla, w_mem, tm, tn):
    m = o_hg.shape[0]
    d = w_hg.shape[1]
    gb = gate_col0 // tn
    nj = d // tn
    return pl.pallas_call(
        _merge_kernel,
        grid=(m // tm, nj),
        in_specs=[pl.BlockSpec((tm, o_hg.shape[1]), lambda i, j: (i, 0)),
                  pl.BlockSpec((tm, o_mla.shape[1]), lambda i, j: (i, 0)),
                  pl.BlockSpec((tm, o_mem.shape[1]), lambda i, j: (i, 0)),
                  pl.BlockSpec((tm, tn), lambda i, j: (i, gb + j)),
                  pl.BlockSpec((tm, tn), lambda i, j: (i, gb + nj + j)),
                  pl.BlockSpec((tm, tn), lambda i, j: (i, gb + 2 * nj + j)),
                  pl.BlockSpec((w_hg.shape[0], tn), lambda i, j: (0, j)),
                  pl.BlockSpec((w_mla.shape[0], tn), lambda i, j: (0, j)),
                  pl.BlockSpec((w_mem.shape[0], tn), lambda i, j: (0, j))],
        out_specs=pl.BlockSpec((tm, tn), lambda i, j: (i, j)),
        out_shape=jax.ShapeDtypeStruct((m, d), BF16),
        compiler_params=_params("parallel", "parallel"),
        name="merge",
    )(o_hg, o_mla, o_mem, z, z, z, w_hg, w_mla, w_mem)


def _out_proj_kernel(m_ref, w_ref, h_ref, g_ref, h1_ref, xn_ref):
    h1 = h_ref[...] + _dot(m_ref[...], w_ref[...])
    h1_ref[...] = h1
    xn = h1 * lax.rsqrt(jnp.mean(h1 * h1, axis=-1, keepdims=True) + EPS) * g_ref[...]
    xn_ref[...] = xn.astype(xn_ref.dtype)


def _out_proj(merged, w_out, h, g_ffn, tm):
    m, d = h.shape
    return pl.pallas_call(
        _out_proj_kernel,
        grid=(m // tm,),
        in_specs=[pl.BlockSpec((tm, d), lambda i: (i, 0)),
                  pl.BlockSpec((d, d), lambda i: (0, 0)),
                  pl.BlockSpec((tm, d), lambda i: (i, 0)),
                  pl.BlockSpec((1, d), lambda i: (0, 0))],
        out_specs=[pl.BlockSpec((tm, d), lambda i: (i, 0)),
                   pl.BlockSpec((tm, d), lambda i: (i, 0))],
        out_shape=[jax.ShapeDtypeStruct((m, d), F32),
                   jax.ShapeDtypeStruct((m, d), BF16)],
        compiler_params=_params("parallel"),
        name="out_proj",
    )(merged, w_out, h, g_ffn.reshape(1, d))


def _router_kernel(h_ref, g_ref, w_ref, b_ref, idx_ref, gate_ref):
    h1 = h_ref[...]
    xn = h1 * lax.rsqrt(jnp.mean(h1 * h1, axis=-1, keepdims=True) + EPS) * g_ref[...]
    logits = jnp.dot(xn, w_ref[...], precision=lax.Precision.HIGHEST, preferred_element_type=F32) + b_ref[...]
    lane = lax.broadcasted_iota(jnp.int32, logits.shape, 1)
    cur = logits
    vals, idxs = [], []
    for _ in range(TOP_K):
        mx = jnp.max(cur, axis=-1, keepdims=True)
        ix = jnp.min(jnp.where(cur == mx, lane, LANE), axis=-1, keepdims=True)
        vals.append(mx)
        idxs.append(ix)
        cur = jnp.where(lane == ix, -jnp.inf, cur)
    es = [jnp.exp(v - vals[0]) for v in vals]
    den = es[0]
    for e in es[1:]:
        den = den + e
    idx_out = jnp.zeros(logits.shape, jnp.int32)
    gate_out = jnp.zeros(logits.shape, F32)
    for k in range(TOP_K):
        idx_out = jnp.where(lane == k, idxs[k], idx_out)
        gate_out = jnp.where(lane == k, es[k] / den, gate_out)
    idx_ref[...] = idx_out
    gate_ref[...] = gate_out


def _router(h1, g_ffn, w_router, b_router, tm):
    m, d = h1.shape
    e = w_router.shape[1]
    w_pad = jnp.zeros((d, LANE), F32).at[:, :e].set(w_router)
    b_pad = jnp.full((1, LANE), -jnp.inf, F32).at[0, :e].set(b_router)
    return pl.pallas_call(
        _router_kernel,
        grid=(m // tm,),
        in_specs=[pl.BlockSpec((tm, d), lambda i: (i, 0)),
                  pl.BlockSpec((1, d), lambda i: (0, 0)),
                  pl.BlockSpec((d, LANE), lambda i: (0, 0)),
                  pl.BlockSpec((1, LANE), lambda i: (0, 0))],
        out_specs=[pl.BlockSpec((tm, LANE), lambda i: (i, 0)),
                   pl.BlockSpec((tm, LANE), lambda i: (i, 0))],
        out_shape=[jax.ShapeDtypeStruct((m, LANE), jnp.int32),
                   jax.ShapeDtypeStruct((m, LANE), F32)],
        compiler_params=_params("parallel"),
        name="router",
    )(h1, g_ffn.reshape(1, d), w_pad, b_pad)


def _moe_kernel(be_ref, bj_ref, bn_ref, x_ref, w1_ref, b1_ref, w2_ref, b2_ref, sel_ref, y_ref, *, sub, n_sub):
    i = pl.program_id(0)
    j = pl.program_id(1)
    n_live = bn_ref[i]

    @pl.when(j == 0)
    def _():
        y_ref[...] = jnp.broadcast_to(b2_ref[...], y_ref.shape)

    @pl.when(n_live > 0)
    def _():
        w1 = w1_ref[...].astype(BF16)
        w2 = w2_ref[...].astype(BF16)
        b1 = b1_ref[...]
        sel = sel_ref[...]
        width = w1.shape[1]
        for s in range(n_sub):
            @pl.when(s * sub < n_live)
            def _():
                rows = pl.ds(s * sub, sub)
                hdn = _dot(x_ref[rows, :], w1) + b1
                lin = pltpu.roll(hdn, width - 1, 1)
                glu = jnp.minimum(hdn, SWIGLU_LIMIT)
                lin = jnp.clip(lin, -SWIGLU_LIMIT, SWIGLU_LIMIT)
                act = glu * _sigmoid(SWIGLU_ALPHA * glu) * (lin + 1.0)
                act = _dot(act.astype(BF16), sel).astype(BF16)
                y_ref[rows, :] += _dot(act, w2)


def _moe(xs, blk_e, blk_j, blk_n, w1, b1, w2, b2):
    n_rows, d = xs.shape
    n_exp, _, two_f = w1.shape
    nf = two_f // (2 * MOE_TF)
    nb = n_rows // MOE_ROWS
    sel = np.zeros((2 * MOE_TF, MOE_TF), np.float32)
    sel[2 * np.arange(MOE_TF), np.arange(MOE_TF)] = 1.0
    grid_spec = pltpu.PrefetchScalarGridSpec(
        num_scalar_prefetch=3,
        grid=(nb, nf),
        in_specs=[pl.BlockSpec((MOE_ROWS, d), lambda i, j, be, bj, bn: (i, 0)),
                  pl.BlockSpec((None, d, 2 * MOE_TF), lambda i, j, be, bj, bn: (be[i], 0, bj[i * nf + j])),
                  pl.BlockSpec((None, 1, 2 * MOE_TF), lambda i, j, be, bj, bn: (be[i], 0, bj[i * nf + j])),
                  pl.BlockSpec((None, MOE_TF, d), lambda i, j, be, bj, bn: (be[i], bj[i * nf + j], 0)),
                  pl.BlockSpec((None, 1, d), lambda i, j, be, bj, bn: (be[i], 0, 0)),
                  pl.BlockSpec((2 * MOE_TF, MOE_TF), lambda i, j, be, bj, bn: (0, 0))],
        out_specs=pl.BlockSpec((MOE_ROWS, d), lambda i, j, be, bj, bn: (i, 0)),
    )
    return pl.pallas_call(
        functools.partial(_moe_kernel, sub=MOE_SUB, n_sub=MOE_ROWS // MOE_SUB),
        grid_spec=grid_spec,
        out_shape=jax.ShapeDtypeStruct((n_rows, d), F32),
        compiler_params=_params("arbitrary", "arbitrary"),
        name="moe_ffn",
    )(blk_e, blk_j, blk_n, xs, w1, b1.reshape(n_exp, 1, two_f), w2, b2.reshape(n_exp, 1, d),
      jnp.asarray(sel, BF16))


def _moe_layout(top_idx, n_exp, nf):
    n_tok = top_idx.shape[0]
    n_assign = n_tok * TOP_K
    flat_e = top_idx.reshape(-1).astype(jnp.int32)
    order = jnp.argsort(flat_e)
    e_sorted = flat_e[order]
    counts = jnp.bincount(flat_e, length=n_exp).astype(jnp.int32)
    nblk = (counts + MOE_ROWS - 1) // MOE_ROWS
    blk_end = jnp.cumsum(nblk)
    blk_start = blk_end - nblk
    start = jnp.cumsum(counts) - counts
    dest = (blk_start[e_sorted] * MOE_ROWS + jnp.arange(n_assign, dtype=jnp.int32) - start[e_sorted]).astype(jnp.int32)
    nb = n_assign // MOE_ROWS + n_exp
    n_rows = nb * MOE_ROWS
    bi = jnp.arange(nb, dtype=jnp.int32)
    used = bi < blk_end[-1]
    be_raw = jnp.minimum(jnp.searchsorted(blk_end, bi, side='right'), n_exp - 1).astype(jnp.int32)
    last_e = be_raw[jnp.maximum(blk_end[-1] - 1, 0)]
    blk_e = jnp.where(used, be_raw, last_e)
    blk_n = jnp.where(used, jnp.clip(counts[blk_e] - (bi - blk_start[blk_e]) * MOE_ROWS, 0, MOE_ROWS), 0).astype(jnp.int32)
    blk_j = jnp.where(used[:, None], jnp.arange(nf, dtype=jnp.int32)[None, :], nf - 1).reshape(-1).astype(jnp.int32)
    tok = jnp.arange(n_assign, dtype=jnp.int32) // TOP_K
    row_tok = jnp.full((n_rows,), n_tok, jnp.int32).at[dest].set(tok[order])
    dest_of_assign = jnp.zeros((n_assign,), jnp.int32).at[order].set(dest)
    return row_tok, dest_of_assign, blk_e, blk_j, blk_n


def _combine_kernel(y_ref, gate_ref, h_ref, g_ref, o_ref):
    gates = gate_ref[...]
    acc = h_ref[...]
    moe = gates[:, 0:1] * y_ref[0]
    for k in range(1, TOP_K):
        moe = moe + gates[:, k:k + 1] * y_ref[k]
    h2 = acc + moe
    o_ref[...] = h2 * lax.rsqrt(jnp.mean(h2 * h2, axis=-1, keepdims=True) + EPS) * g_ref[...]


def _combine(yg, gates, h1, g_final, tm):
    m, d = h1.shape
    return pl.pallas_call(
        _combine_kernel,
        grid=(m // tm,),
        in_specs=[pl.BlockSpec((TOP_K, tm, d), lambda i: (0, i, 0)),
                  pl.BlockSpec((tm, LANE), lambda i: (i, 0)),
                  pl.BlockSpec((tm, d), lambda i: (i, 0)),
                  pl.BlockSpec((1, d), lambda i: (0, 0))],
        out_specs=pl.BlockSpec((tm, d), lambda i: (i, 0)),
        out_shape=jax.ShapeDtypeStruct((m, d), F32),
        compiler_params=_params("parallel"),
        name="moe_combine",
    )(yg, gates, h1, g_final.reshape(1, d))


def _rope_tables(pos, rope, scale_rows=1):
    half = rope // 2
    inv_freq = ROPE_THETA ** (-jnp.arange(half, dtype=F32) / half)
    ang = pos.astype(F32)[:, None] * inv_freq[None, :]
    cos, sin = jnp.cos(ang), jnp.sin(ang)
    zeros = jnp.zeros((pos.shape[0], LANE - rope), F32)
    cos_t = jnp.concatenate([cos, cos, zeros], axis=1)
    sin_t = jnp.concatenate([-sin, sin, zeros], axis=1)
    return cos_t, sin_t


def kernel(x_prompt, x_sample, cache_kv_latent, cache_k_rope, state_hgrn, cache_mem_k, cache_mem_v, page_table, mem_prompt, g_mix, w_in, hg_lb_logits, g_hg_out, g_q_lora, w_uq, g_kv_lora, w_ukv, g_mem, w_mem_kv, w_br_hg, w_br_mla, w_br_mem, w_out, g_ffn, w_router, b_router, w1, b1, w2, b2, g_final):
    B, S, D = x_prompt.shape
    Bd, T, _ = x_sample.shape
    depth = w_in.shape[0]
    q_lora = g_q_lora.shape[1]
    kv_lora = g_kv_lora.shape[1]
    mla_heads = w_uq.shape[2]
    rope = cache_k_rope.shape[-1]
    nope = w_uq.shape[3] - rope
    hg_heads, hg_dk = state_hgrn.shape[2], state_hgrn.shape[3]
    hg_width = hg_heads * hg_dk
    n_mem, mem_heads, mem_dh = cache_mem_k.shape[2:]
    mem_width = mem_heads * mem_dh
    n_exp = w_router.shape[2]
    page = cache_kv_latent.shape[2]
    past_len = page_table.shape[1] * page
    assert nope == LANE and hg_dk == LANE and 2 * rope == LANE and page == LANE

    Tp, Ts = B * S, Bd * T
    mla_scale = float(nope + rope) ** -0.5
    lower_bounds = jnp.cumsum(jax.nn.softmax(hg_lb_logits.astype(F32), axis=0), axis=0)
    cos_p, sin_p = _rope_tables(jnp.arange(S), rope)
    cos_s1, sin_s1 = _rope_tables(past_len + jnp.arange(T), rope)
    cos_s, sin_s = jnp.tile(cos_s1, (Bd, 1)), jnp.tile(sin_s1, (Bd, 1))

    h_p = x_prompt.reshape(Tp, D)
    h_s = x_sample.reshape(Ts, D)
    outs = {k: [] for k in ('lat_p', 'rope_p', 'lat_s', 'rope_s', 'hg_p', 'hg_s', 'mk', 'mv')}

    c_hg = 0
    c_gate = 4 * hg_width
    c_mq = c_gate + 3 * D
    c_mla = c_mq + mem_width

    for l in range(depth):
        wi = w_in[l]
        o_cq, o_ckv = 0, q_lora
        o_kpe = q_lora + kv_lora
        o_hg = o_kpe + rope
        o_mq = o_hg + 4 * hg_width
        o_g = o_mq + mem_width
        kpe_w = wi[:, o_kpe:o_kpe + rope]
        kpe_sw = jnp.concatenate([kpe_w[:, rope // 2:], kpe_w[:, :rope // 2]], axis=1)
        zpad = jnp.zeros((D, LANE - rope), F32)
        w_all = jnp.concatenate([wi[:, o_hg:o_hg + 4 * hg_width], wi[:, o_g:o_g + 3 * D], wi[:, o_mq:o_mq + mem_width],
                                 wi[:, o_cq:o_cq + q_lora], wi[:, o_ckv:o_ckv + kv_lora],
                                 kpe_w, zpad, kpe_sw, zpad], axis=1).astype(BF16)
        n_all = w_all.shape[1]
        tn_in = 896
        assert n_all % tn_in == 0

        wq = w_uq[l]
        zq = jnp.zeros((q_lora, mla_heads, LANE - rope), F32)
        wq_cat = jnp.concatenate([wq, zq], axis=2).reshape(q_lora, mla_heads * 2 * LANE).astype(BF16)
        wq_r = wq[:, :, nope:]
        wq_sw = jnp.concatenate([wq_r[:, :, rope // 2:], wq_r[:, :, :rope // 2], zq], axis=2
                                ).reshape(q_lora, mla_heads * LANE).astype(BF16)
        wkv = w_ukv[l]
        wkv_flat = wkv.reshape(kv_lora, -1).astype(BF16)
        w_uk_h = jnp.transpose(wkv[:, :, :nope], (1, 0, 2)).astype(BF16)
        w_uv_h = jnp.transpose(wkv[:, :, nope:], (1, 0, 2)).astype(BF16)
        w_hg_b, w_mla_b, w_mem_b = w_br_hg[l].astype(BF16), w_br_mla[l].astype(BF16), w_br_mem[l].astype(BF16)
        w_out_b = w_out[l].astype(BF16)
        lb = lower_bounds[l]

        mem_n = _rmsnorm(mem_prompt.reshape(B * n_mem, D), g_mem[l], BF16)
        mem_kv = _matmul(mem_n, w_mem_kv[l].astype(BF16), F32, tm=512, tn=1024)
        outs['mk'].append(mem_kv[:, :mem_width].reshape(B, n_mem, mem_heads, mem_dh))
        outs['mv'].append(mem_kv[:, mem_width:].reshape(B, n_mem, mem_heads, mem_dh))

        a_p = _rmsnorm(h_p, g_mix[l], BF16)
        a_s = _rmsnorm(h_s, g_mix[l], BF16)
        z_p = _matmul(a_p, w_all, F32, tm=1024, tn=tn_in)
        z_s = _matmul(a_s, w_all, F32, tm=512, tn=tn_in)

        cqn_p, ckv_p, ckvb_p, kr_p, krb_p = _mla_prep(z_p, c_mla, g_q_lora[l], g_kv_lora[l], cos_p, sin_p,
                                                      S // 512, 512, q_lora, kv_lora, rope)
        cqn_s, ckv_s, ckvb_s, kr_s, krb_s = _mla_prep(z_s, c_mla, g_q_lora[l], g_kv_lora[l], cos_s, sin_s,
                                                      1, Ts, q_lora, kv_lora, rope)
        outs['lat_p'].append(ckv_p.reshape(B, S, kv_lora))
        outs['rope_p'].append(kr_p.reshape(B, S, rope))
        outs['lat_s'].append(ckv_s.reshape(Bd, T, kv_lora))
        outs['rope_s'].append(kr_s.reshape(Bd, T, rope))

        qcat_p = _q_up(cqn_p, wq_cat, wq_sw, cos_p, sin_p, S // 512, 512, mla_heads, mla_scale)
        qcat_s = _q_up(cqn_s, wq_cat, wq_sw, cos_s, sin_s, 1, Ts, mla_heads, mla_scale)
        kcat_p, v_p = _kv_up(ckvb_p, wkv_flat, krb_p, 512, mla_heads)
        o_mla_p = _flash(qcat_p, kcat_p, v_p, B, S, mla_heads, 512)

        q_abs = _q_lat(qcat_s, w_uk_h, mla_heads, kv_lora)
        q_all = jnp.transpose(q_abs.reshape(mla_heads, Bd, T, kv_lora + LANE), (1, 0, 2, 3)
                              ).reshape(Bd, mla_heads * T, kv_lora + LANE)
        new_lat = jnp.pad(ckvb_s.reshape(Bd, T, kv_lora), ((0, 0), (0, page - T), (0, 0)))
        new_rope = jnp.pad(krb_s.reshape(Bd, T, LANE), ((0, 0), (0, page - T), (0, 0)))
        o_lat = _paged_attention(q_all, cache_kv_latent, cache_k_rope, l, page_table, new_lat, new_rope, T)
        o_lat_h = jnp.transpose(o_lat.reshape(Bd, mla_heads, T, kv_lora), (1, 0, 2, 3)
                                ).reshape(mla_heads, Ts, kv_lora)
        o_mla_s = _head_matmul(o_lat_h, w_uv_h)

        o_hg_p, S_p = _hgrn(z_p, c_hg, lb, g_hg_out[l], None, B, S, hg_heads, HG_CHUNK, 512, 1, HG_CHUNK)
        c_s = 8
        z_s8 = jnp.pad(z_s[:, :4 * hg_width].reshape(Bd, T, 4 * hg_width), ((0, 0), (0, c_s - T), (0, 0))
                       ).reshape(Bd * c_s, 4 * hg_width)
        o_hg_s8, S_s = _hgrn(z_s8, 0, lb, g_hg_out[l], state_hgrn[l].astype(F32), Bd, c_s, hg_heads, c_s, c_s,
                             hg_heads, T)
        o_hg_s = o_hg_s8.reshape(Bd, c_s, hg_width)[:, :T].reshape(Ts, hg_width)
        outs['hg_p'].append(S_p)
        outs['hg_s'].append(S_s)

        o_mem_p = _mem_attn(z_p, c_mq, mem_kv, 0, mem_kv, 1, B, S, n_mem, mem_heads, mem_dh, 512, 1)
        o_mem_s = _mem_attn(z_s, c_mq, cache_mem_k[l].reshape(Bd * n_mem, mem_width), 0,
                            cache_mem_v[l].reshape(Bd * n_mem, mem_width), 0, Bd, T, n_mem, mem_heads, mem_dh, T, 2)

        merged_p = _merge(o_hg_p, o_mla_p, o_mem_p, z_p, c_gate, w_hg_b, w_mla_b, w_mem_b, 1024, 512)
        merged_s = _merge(o_hg_s, o_mla_s, o_mem_s, z_s, c_gate, w_hg_b, w_mla_b, w_mem_b, 512, 512)
        h1_p, xn_p = _out_proj(merged_p, w_out_b, h_p, g_ffn[l], 512)
        h1_s, xn_s = _out_proj(merged_s, w_out_b, h_s, g_ffn[l], 512)

        idx_p, gate_p = _router(h1_p, g_ffn[l], w_router[l], b_router[l], 512)
        idx_s, gate_s = _router(h1_s, g_ffn[l], w_router[l], b_router[l], 512)
        top_idx = jnp.concatenate([idx_p[:, :TOP_K], idx_s[:, :TOP_K]], axis=0)
        nf = w1.shape[3] // (2 * MOE_TF)
        row_tok, dest_of_assign, blk_e, blk_j, blk_n = _moe_layout(top_idx, n_exp, nf)
        xn_all = jnp.concatenate([xn_p, xn_s, jnp.zeros((1, D), BF16)], axis=0)
        xs = xn_all[row_tok]
        ys = _moe(xs, blk_e, blk_j, blk_n, w1[l], b1[l], w2[l], b2[l])
        d2 = dest_of_assign.reshape(Tp + Ts, TOP_K).T
        yg_p = ys[d2[:, :Tp]]
        yg_s = ys[d2[:, Tp:]]
        last = l == depth - 1
        g_fin = g_final if last else jnp.ones((D,), F32)
        assert last, "deeper stacks need an un-normalised combine"
        h_p = _combine(yg_p, gate_p, h1_p, g_fin, 256)
        h_s = _combine(yg_s, gate_s, h1_s, g_fin, 256)

    y_prompt = h_p.reshape(B, S, D)
    y_sample = h_s.reshape(Bd, T, D)
    return (y_prompt, y_sample,
            jnp.stack(outs['lat_p']), jnp.stack(outs['rope_p']),
            jnp.stack(outs['lat_s']), jnp.stack(outs['rope_s']),
            jnp.stack(outs['hg_p']).astype(x_prompt.dtype), jnp.stack(outs['hg_s']).astype(state_hgrn.dtype),
            jnp.stack(outs['mk']), jnp.stack(outs['mv']))
```
